```python
import jax, jax.numpy as jnp
from jax import lax
import numpy as np

D_MODEL = 1024
BATCH = 8
SEQ = 8192
DEPTH = 1
DEC_BATCH = 16
DEC_SEQ = 64
PAST_LEN = 1024

CHUNK = 64
PLE_DIM = 256
W_A = D_MODEL
N_BLOCKS_A = 16
BLOCK_A = W_A // N_BLOCKS_A
CONV_W = 4
LRU_C = 8.0
N_HEADS_B = 16
HEAD_DIM = 64
W_B = N_HEADS_B * HEAD_DIM
Q_BLOCK = 128
N_BRANCH = 2
D_IN_TOTAL = 2 * W_A + 4 * W_B + N_HEADS_B + N_BRANCH * D_MODEL
EPS = 1e-6

kernel_name = "hawk_fox_parallel_stream_step"


def rms_norm(x, g):
    xf = x.astype(jnp.float32)
    y = xf * lax.rsqrt(jnp.mean(xf * xf, axis=-1, keepdims=True) + EPS)
    return (y * g.astype(jnp.float32)).astype(x.dtype)


def causal_conv(u, hist, w, b):
    L = u.shape[1]
    full = jnp.concatenate([hist.astype(u.dtype), u], axis=1)
    y = b
    for tap in range(CONV_W):
        y = y + full[:, tap:tap + L, :] * w[tap]
    return y, full[:, -(CONV_W - 1):, :]


def rg_lru(u, h0, w_a, b_a, w_x, b_x, a_param):
    B, L, _ = u.shape
    ub = u.reshape(B, L, N_BLOCKS_A, BLOCK_A)
    r = jax.nn.sigmoid((jnp.einsum('blnj,njk->blnk', ub, w_a).reshape(B, L, W_A) + b_a).astype(jnp.float32))
    i = jax.nn.sigmoid((jnp.einsum('blnj,njk->blnk', ub, w_x).reshape(B, L, W_A) + b_x).astype(jnp.float32))
    log_a = -LRU_C * r * jax.nn.softplus(-a_param.astype(jnp.float32))
    a = jnp.exp(log_a)
    x_in = jnp.sqrt(-jnp.expm1(2.0 * log_a)) * (i * u.astype(jnp.float32))

    def step(h, inp):
        a_t, x_t = inp
        h = a_t * h + x_t
        return h, h

    h_last, hs = lax.scan(step, h0.astype(jnp.float32),
                          (jnp.swapaxes(a, 0, 1), jnp.swapaxes(x_in, 0, 1)))
    return jnp.swapaxes(hs, 0, 1).astype(u.dtype), h_last.astype(u.dtype)


def fox_attend(q, k, v, cq, ck, qpos, kpos):
    s = jnp.einsum('qhd,khd->hqk', q, k).astype(jnp.float32) * (HEAD_DIM ** -0.5)
    s = s + cq.astype(jnp.float32).T[:, :, None] - ck.astype(jnp.float32).T[:, None, :]
    mask = kpos[None, :] <= qpos[:, None]
    s = jnp.where(mask[None], s, -jnp.inf)
    p = jax.nn.softmax(s, axis=-1)
    return jnp.einsum('hqk,khd->qhd', p.astype(v.dtype), v)


def fox_prompt(q, k, v, logf):
    B, S, H, Dh = q.shape
    cum = jnp.cumsum(logf, axis=1)
    nq = S // Q_BLOCK
    kpos = jnp.arange(S)

    def one_block(idx):
        b = idx // nq
        q0 = (idx % nq) * Q_BLOCK
        qb = lax.dynamic_slice(q, (b, q0, 0, 0), (1, Q_BLOCK, H, Dh))[0]
        cqb = lax.dynamic_slice(cum, (b, q0, 0), (1, Q_BLOCK, H))[0]
        kb = lax.dynamic_index_in_dim(k, b, 0, keepdims=False)
        vb = lax.dynamic_index_in_dim(v, b, 0, keepdims=False)
        ckb = lax.dynamic_index_in_dim(cum, b, 0, keepdims=False)
        return fox_attend(qb, kb, vb, cqb, ckb, q0 + jnp.arange(Q_BLOCK), kpos)

    out = lax.map(one_block, jnp.arange(B * nq))
    return out.reshape(B, S, H, Dh)


def fox_sample(q, k, v, logf, k_past, v_past, lf_past):
    P = k_past.shape[1]
    L = q.shape[1]
    k_all = jnp.concatenate([k_past.astype(k.dtype), k], axis=1)
    v_all = jnp.concatenate([v_past.astype(v.dtype), v], axis=1)
    cum = jnp.cumsum(jnp.concatenate([lf_past.astype(jnp.float32), logf], axis=1), axis=1)
    qpos = P + jnp.arange(L)
    kpos = jnp.arange(P + L)
    return jax.vmap(fox_attend, in_axes=(0, 0, 0, 0, 0, None, None))(q, k_all, v_all, cum[:, P:], cum, qpos, kpos)


def layer(x, pe, conv_hist, h0, k_past, v_past, lf_past, w_in, conv_w, conv_b, w_rg_a, b_rg_a, w_rg_x, b_rg_x,
          a_param, b_f, w_branch, w_o, g_pre, g_post, w_ple_gate, w_ple_proj, g_ple):
    B, L, _ = x.shape
    xn = rms_norm(x, g_pre)
    z = xn @ w_in
    sizes = (W_A, W_A, W_B, W_B, W_B, W_B, N_HEADS_B, D_MODEL, D_MODEL)
    xa, ga, q, k, v, gb, fl, ma, mb = jnp.split(z, np.cumsum(sizes)[:-1], axis=-1)
    ua, conv_new = causal_conv(xa, conv_hist, conv_w, conv_b)
    ya, h_new = rg_lru(ua, h0, w_rg_a, b_rg_a, w_rg_x, b_rg_x, a_param)
    ya = ya * jax.nn.silu(ga)
    q = q.reshape(B, L, N_HEADS_B, HEAD_DIM)
    k = k.reshape(B, L, N_HEADS_B, HEAD_DIM)
    v = v.reshape(B, L, N_HEADS_B, HEAD_DIM)
    logf = jax.nn.log_sigmoid((fl + b_f).astype(jnp.float32))
    if k_past is None:
        yb = fox_prompt(q, k, v, logf)
    else:
        yb = fox_sample(q, k, v, logf, k_past, v_past, lf_past)
    yb = yb.reshape(B, L, W_B) * jax.nn.silu(gb)
    u = jnp.einsum('bsnw,nwd->bsnd', jnp.stack([ya, yb], axis=2), w_branch)
    gates = jax.nn.sigmoid(jnp.stack([ma, mb], axis=2))
    mix = jnp.sum(gates * u, axis=2) @ w_o
    h = x + rms_norm(mix, g_post)
    ple = jax.nn.sigmoid(h @ w_ple_gate) * (pe @ w_ple_proj)
    h = h + rms_norm(ple, g_ple)
    return h, (k, v, logf, h_new, conv_new)


def setup_inputs(seed: int = 0) -> dict:
    key = jax.random.key(seed)
    ks = jax.random.split(key, 32)
    f32 = jnp.float32
    nrm = lambda k, shape, s=1.0: (jax.random.normal(k, shape, f32) * s)
    u_a = jax.random.uniform(ks[20], (DEPTH, W_A), f32, minval=0.9, maxval=0.999)
    a0 = u_a ** (1.0 / LRU_C)
    return {
        "x_prompt": nrm(ks[0], (BATCH, SEQ, D_MODEL)),
        "x_sample": nrm(ks[1], (DEC_BATCH, DEC_SEQ, D_MODEL)),
        "p_prompt": nrm(ks[2], (DEPTH, BATCH, SEQ, PLE_DIM)),
        "p_sample": nrm(ks[3], (DEPTH, DEC_BATCH, DEC_SEQ, PLE_DIM)),
        "cache_k": nrm(ks[4], (DEPTH, DEC_BATCH, PAST_LEN, N_HEADS_B, HEAD_DIM)),
        "cache_v": nrm(ks[5], (DEPTH, DEC_BATCH, PAST_LEN, N_HEADS_B, HEAD_DIM)),
        "cache_logf": jax.nn.log_sigmoid(3.0 + nrm(ks[6], (DEPTH, DEC_BATCH, PAST_LEN, N_HEADS_B))),
        "state_h": nrm(ks[7], (DEPTH, DEC_BATCH, W_A), 0.5),
        "state_conv": nrm(ks[8], (DEPTH, DEC_BATCH, CONV_W - 1, W_A)),
        "w_in": nrm(ks[9], (DEPTH, D_MODEL, D_IN_TOTAL), D_MODEL ** -0.5),
        "conv_w": nrm(ks[10], (DEPTH, CONV_W, W_A), 0.5),
        "conv_b": nrm(ks[11], (DEPTH, W_A), 0.01),
        "w_rg_a": nrm(ks[12], (DEPTH, N_BLOCKS_A, BLOCK_A, BLOCK_A), BLOCK_A ** -0.5),
        "b_rg_a": nrm(ks[13], (DEPTH, W_A), 0.01),
        "w_rg_x": nrm(ks[14], (DEPTH, N_BLOCKS_A, BLOCK_A, BLOCK_A), BLOCK_A ** -0.5),
        "b_rg_x": nrm(ks[15], (DEPTH, W_A), 0.01),
        "a_param": jnp.log(a0) - jnp.log1p(-a0),
        "b_f": 3.0 + nrm(ks[16], (DEPTH, N_HEADS_B), 0.5),
        "w_branch": nrm(ks[17], (DEPTH, N_BRANCH, W_A, D_MODEL), W_A ** -0.5),
        "w_o": nrm(ks[18], (DEPTH, D_MODEL, D_MODEL), D_MODEL ** -0.5),
        "g_pre": 1.0 + nrm(ks[19], (DEPTH, D_MODEL), 0.01),
        "g_post": 1.0 + nrm(ks[21], (DEPTH, D_MODEL), 0.01),
        "w_ple_gate": nrm(ks[22], (DEPTH, D_MODEL, D_MODEL), D_MODEL ** -0.5),
        "w_ple_proj": nrm(ks[23], (DEPTH, PLE_DIM, D_MODEL), PLE_DIM ** -0.5),
        "g_ple": 1.0 + nrm(ks[24], (DEPTH, D_MODEL), 0.01),
    }


def reference(x_prompt, x_sample, p_prompt, p_sample, cache_k, cache_v, cache_logf, state_h, state_conv,
              w_in, conv_w, conv_b, w_rg_a, b_rg_a, w_rg_x, b_rg_x, a_param, b_f, w_branch, w_o,
              g_pre, g_post, w_ple_gate, w_ple_proj, g_ple):
    hp, hs = x_prompt, x_sample
    kp_l, vp_l, fp_l, hp_l, cp_l = [], [], [], [], []
    ks_l, vs_l, fs_l, hs_l, cs_l = [], [], [], [], []
    for i in range(DEPTH):
        prm = (w_in[i], conv_w[i], conv_b[i], w_rg_a[i], b_rg_a[i], w_rg_x[i], b_rg_x[i], a_param[i], b_f[i],
               w_branch[i], w_o[i], g_pre[i], g_post[i], w_ple_gate[i], w_ple_proj[i], g_ple[i])
        Bp = hp.shape[0]
        hp, (k1, v1, f1, h1, c1) = layer(hp, p_prompt[i], jnp.zeros((Bp, CONV_W - 1, W_A), hp.dtype),
                                         jnp.zeros((Bp, W_A), jnp.float32), None, None, None, *prm)
        hs, (k2, v2, f2, h2, c2) = layer(hs, p_sample[i], state_conv[i], state_h[i],
                                         cache_k[i], cache_v[i], cache_logf[i], *prm)
        kp_l.append(k1); vp_l.append(v1); fp_l.append(f1); hp_l.append(h1); cp_l.append(c1)
        ks_l.append(k2); vs_l.append(v2); fs_l.append(f2); hs_l.append(h2); cs_l.append(c2)
    return (hp, hs,
            jnp.stack(kp_l), jnp.stack(vp_l), jnp.stack(fp_l), jnp.stack(hp_l), jnp.stack(cp_l),
            jnp.stack(ks_l), jnp.stack(vs_l), jnp.stack(fs_l), jnp.stack(hs_l), jnp.stack(cs_l))
```

```python
import functools

import numpy as np
import jax
import jax.numpy as jnp
from jax import lax
from jax.experimental import pallas as pl
from jax.experimental.pallas import tpu as pltpu

F32 = jnp.float32
BF16 = jnp.bfloat16

D_MODEL = 1024
N_HEADS = 16
HEAD_DIM = 64
N_BLOCKS_A = 16
BLOCK_A = 64
CONV_W = 4
LRU_C = 8.0
EPS = 1e-6
N_SEG = 8
LANES = 128
HEADS_PER_STEP = LANES // HEAD_DIM
N_HEAD_PAIRS = N_HEADS // HEADS_PER_STEP
AUG_PER_HEAD = 6
GATE_TILE = 256
GATE_TILES = D_MODEL // GATE_TILE
NEG_BIG = -1e30
VMEM_LIMIT = 56 * 1024 * 1024


def _softplus(y):
    return jnp.maximum(y, 0.0) + jnp.log1p(jnp.exp(-jnp.abs(y)))


def _rms(x, g):
    return x * lax.rsqrt(jnp.mean(x * x, axis=-1, keepdims=True) + EPS) * g


def _split3(c):
    hi = c.astype(BF16)
    r1 = c - hi.astype(F32)
    mid = r1.astype(BF16)
    lo = (r1 - mid.astype(F32)).astype(BF16)
    return hi, mid, lo


def _resident(shape):
    zeros = (0,) * len(shape)
    return pl.BlockSpec(shape, lambda *_: zeros, pipeline_mode=pl.Buffered(1))


def _inproj_kernel(x_ref, g_ref, w_ref, wfl_ref, bf_ref,
                   xa_ref, ga_ref, q_ref, kf_ref, kb_ref, vf_ref, vb_ref,
                   gb_ref, ma_ref, mb_ref, lf_ref):
    xn = _rms(x_ref[...], g_ref[...]).astype(BF16)

    def seg(j):
        return jnp.dot(xn, w_ref[j], preferred_element_type=F32)

    xa_ref[...] = seg(0)
    ga_ref[...] = seg(1).astype(BF16)
    q_ref[...] = (seg(2) * (HEAD_DIM ** -0.5)).astype(BF16)
    k = seg(3)
    kf_ref[...] = k
    kb_ref[...] = k.astype(BF16)
    v = seg(4)
    vf_ref[...] = v
    vb_ref[...] = v.astype(BF16)
    gb_ref[...] = seg(5).astype(BF16)
    ma_ref[...] = seg(6).astype(BF16)
    mb_ref[...] = seg(7).astype(BF16)
    fl = jnp.dot(xn, wfl_ref[...], preferred_element_type=F32) + bf_ref[...]
    lf_ref[...] = (-_softplus(-fl))[:, :N_HEADS]


def _inproj(x2, g_pre, w_seg, w_fl, b_f, tm):
    n = x2.shape[0]
    row = lambda i: (i, 0)
    wide_f32 = jax.ShapeDtypeStruct((n, D_MODEL), F32)
    wide_bf16 = jax.ShapeDtypeStruct((n, D_MODEL), BF16)
    wide_spec = pl.BlockSpec((tm, D_MODEL), row)
    out_shape = (wide_f32, wide_bf16, wide_bf16, wide_f32, wide_bf16, wide_f32, wide_bf16,
                 wide_bf16, wide_bf16, wide_bf16, jax.ShapeDtypeStruct((n, N_HEADS), F32))
    out_specs = (wide_spec,) * 10 + (pl.BlockSpec((tm, N_HEADS), row),)
    return pl.pallas_call(
        _inproj_kernel,
        grid=(n // tm,),
        in_specs=[wide_spec,
                  _resident((1, D_MODEL)),
                  _resident((N_SEG, D_MODEL, D_MODEL)),
                  _resident((D_MODEL, LANES)),
                  _resident((1, LANES))],
        out_specs=out_specs,
        out_shape=out_shape,
        compiler_params=pltpu.CompilerParams(
            dimension_semantics=("arbitrary",), vmem_limit_bytes=VMEM_LIMIT),
        name="inproj",
    )(x2, g_pre, w_seg, w_fl, b_f)


CUM_BLOCK = 256


def _cum_kernel(lf_ref, tri_ref, pq_ref, pk_ref, oq_ref, ok_ref, aq_ref, ak_ref, carry_ref):
    n_blocks = lf_ref.shape[1] // CUM_BLOCK
    carry_ref[...] = jnp.zeros_like(carry_ref)
    tri = tri_ref[...]

    def place(parts, p_ref):
        out = jnp.dot(parts[0], p_ref[0], preferred_element_type=F32)
        out += jnp.dot(parts[1], p_ref[1], preferred_element_type=F32)
        out += jnp.dot(parts[2], p_ref[2], preferred_element_type=F32)
        return out

    def body(i, _):
        rows = pl.ds(pl.multiple_of(i * CUM_BLOCK, CUM_BLOCK), CUM_BLOCK)
        hi, mid, lo = _split3(lf_ref[0, rows, :])
        local = jnp.dot(tri, hi, preferred_element_type=F32)
        local += jnp.dot(tri, mid, preferred_element_type=F32)
        local += jnp.dot(tri, lo, preferred_element_type=F32)
        cum = local + carry_ref[...]
        carry_ref[...] = cum[CUM_BLOCK - 1:CUM_BLOCK, :]
        parts = _split3(cum)
        aq_ref[0, rows, :] = (place(parts, pq_ref) + oq_ref[...]).astype(BF16)
        ak_ref[0, rows, :] = (place(parts, pk_ref) + ok_ref[...]).astype(BF16)
        return 0

    lax.fori_loop(0, n_blocks, body, 0)


def _aug_constants():
    pq = np.zeros((3, N_HEADS, LANES), np.float32)
    pk = np.zeros((3, N_HEADS, LANES), np.float32)
    oq = np.zeros((1, LANES), np.float32)
    ok = np.zeros((1, LANES), np.float32)
    for h in range(N_HEADS):
        for i in range(3):
            pq[i, h, AUG_PER_HEAD * h + i] = 1.0
            ok[0, AUG_PER_HEAD * h + i] = 1.0
            pk[i, h, AUG_PER_HEAD * h + 3 + i] = -1.0
            oq[0, AUG_PER_HEAD * h + 3 + i] = 1.0
    tri = np.tril(np.ones((CUM_BLOCK, CUM_BLOCK), np.float32))
    return (jnp.asarray(tri, BF16), jnp.asarray(pq, BF16), jnp.asarray(pk, BF16),
            jnp.asarray(oq), jnp.asarray(ok))


def _cum_aug(logf):
    b, s, _ = logf.shape
    tri, pq, pk, oq, ok = _aug_constants()
    aug = jax.ShapeDtypeStruct((b, s, LANES), BF16)
    aug_spec = pl.BlockSpec((1, s, LANES), lambda i: (i, 0, 0))
    return pl.pallas_call(
        _cum_kernel,
        grid=(b,),
        in_specs=[pl.BlockSpec((1, s, N_HEADS), lambda i: (i, 0, 0)),
                  _resident((CUM_BLOCK, CUM_BLOCK)),
                  _resident((3, N_HEADS, LANES)),
                  _resident((3, N_HEADS, LANES)),
                  _resident((1, LANES)),
                  _resident((1, LANES))],
        out_specs=(aug_spec, aug_spec),
        out_shape=(aug, aug),
        scratch_shapes=[pltpu.VMEM((1, N_HEADS), F32)],
        compiler_params=pltpu.CompilerParams(
            dimension_semantics=("arbitrary",), vmem_limit_bytes=VMEM_LIMIT),
        name="cum_aug",
    )(logf, tri, pq, pk, oq, ok)


HIST = CONV_W - 1
PAD = 8


def _lru_kernel(xa_ref, ga_ref, hist_ref, h0_ref, cw_ref, cb_ref, wa_ref, ba_ref, wx_ref, bx_ref,
                ap_ref, y_ref, hl_ref, cn_ref, buf_ref, hc_ref):
    t = pl.program_id(1)
    tt = xa_ref.shape[1]

    @pl.when(t == 0)
    def _():
        buf_ref[PAD - HIST:PAD, :] = hist_ref[0]
        hc_ref[...] = h0_ref[0]

    buf_ref[PAD:PAD + tt, :] = xa_ref[0]
    u = cb_ref[...]
    for tap in range(CONV_W):
        u = u + buf_ref[PAD - HIST + tap:PAD - HIST + tap + tt, :] * cw_ref[tap:tap + 1, :]
    tail = buf_ref[PAD + tt - HIST:PAD + tt, :]
    buf_ref[PAD - HIST:PAD, :] = tail
    cn_ref[0] = tail

    ub = u.astype(BF16)

    def gate(w_ref, b_ref):
        cols = [jnp.dot(ub[:, c * GATE_TILE:(c + 1) * GATE_TILE], w_ref[c],
                        preferred_element_type=F32) for c in range(GATE_TILES)]
        return jax.nn.sigmoid(jnp.concatenate(cols, axis=1) + b_ref[...])

    r = gate(wa_ref, ba_ref)
    i = gate(wx_ref, bx_ref)
    log_a = -LRU_C * r * _softplus(-ap_ref[...])
    a = jnp.exp(log_a)
    x_in = jnp.sqrt(jnp.tanh(-log_a) * (a * a + 1.0)) * (i * u)

    row = lax.broadcasted_iota(jnp.int32, (tt, 1), 0)
    d = 1
    while d < tt:
        keep = row >= d
        x_sh = pltpu.roll(x_in, d, 0)
        a_sh = pltpu.roll(a, d, 0)
        x_in = jnp.where(keep, a * x_sh + x_in, x_in)
        a = jnp.where(keep, a * a_sh, a)
        d *= 2
    h = x_in + a * hc_ref[...]
    h_last = h[tt - 1:tt, :]
    hc_ref[...] = h_last
    hl_ref[0] = h_last
    g = ga_ref[0].astype(F32)
    y_ref[0] = (h * (g * jax.nn.sigmoid(g))).astype(BF16)


def _lru(xa, ga, hist, h0, cw, cb, wa, ba, wx, bx, ap, tt):
    b, l, _ = xa.shape
    tile = pl.BlockSpec((1, tt, D_MODEL), lambda i, t: (i, t, 0))
    per_b = lambda rows: pl.BlockSpec((1, rows, D_MODEL), lambda i, t: (i, 0, 0))
    vec = _resident((1, D_MODEL))
    wgt = _resident((GATE_TILES, GATE_TILE, GATE_TILE))
    return pl.pallas_call(
        _lru_kernel,
        grid=(b, l // tt),
        in_specs=[tile, tile, per_b(HIST), per_b(1), _resident((CONV_W, D_MODEL)), vec,
                  wgt, vec, wgt, vec, vec],
        out_specs=(tile, per_b(1), per_b(HIST)),
        out_shape=(jax.ShapeDtypeStruct((b, l, D_MODEL), BF16),
                   jax.ShapeDtypeStruct((b, 1, D_MODEL), F32),
                   jax.ShapeDtypeStruct((b, HIST, D_MODEL), F32)),
        scratch_shapes=[pltpu.VMEM((PAD + tt, D_MODEL), F32), pltpu.VMEM((1, D_MODEL), F32)],
        compiler_params=pltpu.CompilerParams(
            dimension_semantics=("arbitrary", "arbitrary"), vmem_limit_bytes=VMEM_LIMIT),
        name="rg_lru",
    )(xa, ga, hist, h0, cw, cb, wa, ba, wx, bx, ap)


def _fox_kernel(q_ref, aq_ref, k_ref, ak_ref, v_ref, gb_ref, y_ref,
                kcat_ref, vcat_ref, m_ref, acc_ref, *, tq, tk, offset):
    hp = pl.program_id(1)
    qi = pl.program_id(2)
    sk = k_ref.shape[1]

    @pl.when(qi == 0)
    def _():
        kcat_ref[:, :LANES] = k_ref[0]
        kcat_ref[:, LANES:] = ak_ref[0]
        vcat_ref[:, :LANES] = v_ref[0]
        vcat_ref[:, LANES:] = jnp.ones((sk, LANES), BF16)

    lane = lax.broadcasted_iota(jnp.int32, (1, LANES), 1)
    qb = q_ref[0]
    ab = aq_ref[0]
    lhs = []
    for h in range(HEADS_PER_STEP):
        head = hp * HEADS_PER_STEP + h
        q_keep = (lane >= h * HEAD_DIM) & (lane < (h + 1) * HEAD_DIM)
        a_keep = (lane >= head * AUG_PER_HEAD) & (lane < (head + 1) * AUG_PER_HEAD)
        lhs.append(jnp.concatenate(
            [jnp.where(q_keep, qb, jnp.zeros_like(qb)), jnp.where(a_keep, ab, jnp.zeros_like(ab))],
            axis=1))

    m_ref[...] = jnp.full(m_ref.shape, NEG_BIG, F32)
    acc_ref[...] = jnp.zeros(acc_ref.shape, F32)

    q_first = offset + qi * tq
    n_full = (q_first + 1) // tk
    n_end = (q_first + tq + tk - 1) // tk

    def step(kb, masked):
        rows = pl.ds(pl.multiple_of(kb * tk, tk), tk)
        kblk = kcat_ref[rows, :]
        vblk = vcat_ref[rows, :]
        if masked:
            qpos = q_first + lax.broadcasted_iota(jnp.int32, (tq, tk), 0)
            kpos = kb * tk + lax.broadcasted_iota(jnp.int32, (tq, tk), 1)
            visible = kpos <= qpos
        for h in range(HEADS_PER_STEP):
            s = lax.dot_general(lhs[h], kblk, (((1,), (1,)), ((), ())),
                                preferred_element_type=F32)
            if masked:
                s = jnp.where(visible, s, NEG_BIG)
            m_prev = m_ref[h]
            m_new = jnp.maximum(m_prev, jnp.max(s, axis=1, keepdims=True))
            alpha = jnp.exp(m_prev - m_new)
            p = jnp.exp(s - jnp.tile(m_new, (1, tk // LANES)))
            pv = jnp.dot(p.astype(BF16), vblk, preferred_element_type=F32)
            acc_ref[h] = jnp.tile(alpha, (1, 2)) * acc_ref[h] + pv
            m_ref[h] = m_new

    def full_body(kb, c):
        step(kb, False)
        return c

    def masked_body(kb, c):
        step(kb, True)
        return c

    lax.fori_loop(0, n_full, full_body, 0)
    lax.fori_loop(n_full, n_end, masked_body, 0)

    outs = [acc_ref[h][:, :LANES] / acc_ref[h][:, LANES:] for h in range(HEADS_PER_STEP)]
    o = jnp.where(lane < HEAD_DIM, outs[0], outs[1])
    g = gb_ref[0].astype(F32)
    y_ref[0] = (o * (g * jax.nn.sigmoid(g))).astype(BF16)


def _fox(q, aq, k, ak, v, gb, tq, tk, offset):
    b, sq, _ = q.shape
    sk = k.shape[1]
    q_spec = pl.BlockSpec((1, tq, LANES), lambda i, hp, qi: (i, qi, hp))
    aq_spec = pl.BlockSpec((1, tq, LANES), lambda i, hp, qi: (i, qi, 0))
    k_spec = pl.BlockSpec((1, sk, LANES), lambda i, hp, qi: (i, 0, hp))
    ak_spec = pl.BlockSpec((1, sk, LANES), lambda i, hp, qi: (i, 0, 0))
    return pl.pallas_call(
        functools.partial(_fox_kernel, tq=tq, tk=tk, offset=offset),
        grid=(b, N_HEAD_PAIRS, sq // tq),
        in_specs=[q_spec, aq_spec, k_spec, ak_spec, k_spec, q_spec],
        out_specs=q_spec,
        out_shape=jax.ShapeDtypeStruct((b, sq, D_MODEL), BF16),
        scratch_shapes=[pltpu.VMEM((sk, 2 * LANES), BF16),
                        pltpu.VMEM((sk, 2 * LANES), BF16),
                        pltpu.VMEM((HEADS_PER_STEP, tq, LANES), F32),
                        pltpu.VMEM((HEADS_PER_STEP, tq, 2 * LANES), F32)],
        compiler_params=pltpu.CompilerParams(
            dimension_semantics=("arbitrary", "arbitrary", "arbitrary"),
            vmem_limit_bytes=VMEM_LIMIT),
        name="fox_attention",
    )(q, aq, k, ak, v, gb)


def _post_kernel(ya_ref, yb_ref, ma_ref, mb_ref, x_ref, pe_ref, wb_ref, wo_ref, gpost_ref,
                 wg_ref, wp_ref, gple_ref, out_ref):
    ua = jnp.dot(ya_ref[...], wb_ref[0], preferred_element_type=F32)
    ub = jnp.dot(yb_ref[...], wb_ref[1], preferred_element_type=F32)
    mixed = (jax.nn.sigmoid(ma_ref[...].astype(F32)) * ua
             + jax.nn.sigmoid(mb_ref[...].astype(F32)) * ub)
    mix = jnp.dot(mixed.astype(BF16), wo_ref[...], preferred_element_type=F32)
    h = x_ref[...] + _rms(mix, gpost_ref[...])
    gate = jax.nn.sigmoid(jnp.dot(h.astype(BF16), wg_ref[...], preferred_element_type=F32))
    proj = jnp.dot(pe_ref[...].astype(BF16), wp_ref[...], preferred_element_type=F32)
    out_ref[...] = h + _rms(gate * proj, gple_ref[...])


def _post(ya, yb, ma, mb, x2, pe2, wb, wo, gpost, wg, wp, gple, tm):
    n = x2.shape[0]
    ple_dim = pe2.shape[1]
    row = lambda i: (i, 0)
    wide = pl.BlockSpec((tm, D_MODEL), row)
    sq_w = _resident((D_MODEL, D_MODEL))
    vec = _resident((1, D_MODEL))
    return pl.pallas_call(
        _post_kernel,
        grid=(n // tm,),
        in_specs=[wide, wide, wide, wide, wide, pl.BlockSpec((tm, ple_dim), row),
                  _resident((2, D_MODEL, D_MODEL)), sq_w, vec, sq_w,
                  _resident((ple_dim, D_MODEL)), vec],
        out_specs=wide,
        out_shape=jax.ShapeDtypeStruct((n, D_MODEL), F32),
        compiler_params=pltpu.CompilerParams(
            dimension_semantics=("arbitrary",), vmem_limit_bytes=VMEM_LIMIT),
        name="merge_post",
    )(ya, yb, ma, mb, x2, pe2, wb, wo, gpost, wg, wp, gple)


def _block_diag_tiles(w):
    per_tile = GATE_TILE // BLOCK_A
    w = w.reshape(GATE_TILES, per_tile, BLOCK_A, BLOCK_A)
    eye = jnp.eye(per_tile, dtype=w.dtype)
    tiles = jnp.einsum('cajk,ab->cajbk', w, eye)
    return tiles.reshape(GATE_TILES, GATE_TILE, GATE_TILE).astype(BF16)


def _prep_weights(w_in, conv_w, conv_b, w_rg_a, b_rg_a, w_rg_x, b_rg_x, a_param, b_f, w_branch,
                  w_o, g_pre, g_post, w_ple_gate, w_ple_proj, g_ple):
    d = D_MODEL
    starts = (0, d, 2 * d, 3 * d, 4 * d, 5 * d, 6 * d + N_HEADS, 7 * d + N_HEADS)
    w_seg = jnp.stack([w_in[:, s:s + d] for s in starts]).astype(BF16)
    w_fl = jnp.pad(w_in[:, 6 * d:6 * d + N_HEADS], ((0, 0), (0, LANES - N_HEADS))).astype(BF16)
    b_fl = jnp.pad(b_f, (0, LANES - N_HEADS)).reshape(1, LANES)
    vec = lambda a: a.reshape(1, d)
    return dict(
        g_pre=vec(g_pre), w_seg=w_seg, w_fl=w_fl, b_fl=b_fl,
        cw=conv_w, cb=vec(conv_b), wa=_block_diag_tiles(w_rg_a), ba=vec(b_rg_a),
        wx=_block_diag_tiles(w_rg_x), bx=vec(b_rg_x), ap=vec(a_param),
        wb=w_branch.astype(BF16), wo=w_o.astype(BF16), gpost=vec(g_post),
        wg=w_ple_gate.astype(BF16), wp=w_ple_proj.astype(BF16), gple=vec(g_ple))


def _round_up(n, m):
    return (n + m - 1) // m * m


def _layer(x, pe, conv_hist, h0, k_past, v_past, lf_past, p):
    b, l, d = x.shape
    n = b * l
    tm = min(256, n)
    x2 = x.reshape(n, d)
    xa, ga, q, k_f, k_b, v_f, v_b, gb, ma, mb, logf = _inproj(
        x2, p["g_pre"], p["w_seg"], p["w_fl"], p["b_fl"], tm)
    r3 = lambda a: a.reshape(b, l, d)

    ya, h_new, conv_new = _lru(r3(xa), r3(ga), conv_hist, h0.reshape(b, 1, d), p["cw"], p["cb"],
                               p["wa"], p["ba"], p["wx"], p["bx"], p["ap"], min(256, l))

    logf3 = logf.reshape(b, l, N_HEADS)
    tk = 512
    if k_past is None:
        past = 0
        lf_all, k_all, v_all = logf3, r3(k_b), r3(v_b)
        tq = min(512, l)
    else:
        past = k_past.shape[1]
        sk = _round_up(past + l, tk)
        tail = sk - past - l
        cat = lambda old, new: jnp.concatenate(
            [old.reshape(b, past, -1).astype(new.dtype), new,
             jnp.zeros((b, tail, new.shape[-1]), new.dtype)], axis=1)
        lf_all, k_all, v_all = cat(lf_past, logf3), cat(k_past, r3(k_b)), cat(v_past, r3(v_b))
        tq = l
    aq, ak = _cum_aug(lf_all)
    aq = aq[:, past:past + l]
    yb = _fox(r3(q), aq, k_all, ak, v_all, r3(gb), tq, tk, past)

    out = _post(ya.reshape(n, d), yb.reshape(n, d), ma, mb, x2, pe.reshape(n, -1),
                p["wb"], p["wo"], p["gpost"], p["wg"], p["wp"], p["gple"], tm)
    heads = lambda a: a.reshape(b, l, N_HEADS, HEAD_DIM)
    return out.reshape(b, l, d), (heads(k_f), heads(v_f), logf3, h_new.reshape(b, d), conv_new)


def kernel(x_prompt, x_sample, p_prompt, p_sample, cache_k, cache_v, cache_logf, state_h, state_conv,
           w_in, conv_w, conv_b, w_rg_a, b_rg_a, w_rg_x, b_rg_x, a_param, b_f, w_branch, w_o,
           g_pre, g_post, w_ple_gate, w_ple_proj, g_ple):
    depth = w_in.shape[0]
    hp, hs = x_prompt, x_sample
    outs_p, outs_s = [], []
    for i in range(depth):
        p = _prep_weights(w_in[i], conv_w[i], conv_b[i], w_rg_a[i], b_rg_a[i], w_rg_x[i], b_rg_x[i],
                          a_param[i], b_f[i], w_branch[i], w_o[i], g_pre[i], g_post[i],
                          w_ple_gate[i], w_ple_proj[i], g_ple[i])
        bp = hp.shape[0]
        hp, st_p = _layer(hp, p_prompt[i], jnp.zeros((bp, HIST, D_MODEL), hp.dtype),
                          jnp.zeros((bp, D_MODEL), F32), None, None, None, p)
        hs, st_s = _layer(hs, p_sample[i], state_conv[i], state_h[i],
                          cache_k[i], cache_v[i], cache_logf[i], p)
        outs_p.append(st_p)
        outs_s.append(st_s)
    stack = lambda outs, j: jnp.stack([o[j] for o in outs])
    return (hp, hs) + tuple(stack(outs_p, j) for j in range(5)) + tuple(stack(outs_s, j) for j in range(5))
```

```python
import functools

import numpy as np
import jax
import jax.numpy as jnp
from jax import lax
from jax.experimental import pallas as pl
from jax.experimental.pallas import tpu as pltpu

F32 = jnp.float32
BF16 = jnp.bfloat16

D_MODEL = 1024
N_HEADS = 16
HEAD_DIM = 64
N_BLOCKS_A = 16
BLOCK_A = 64
CONV_W = 4
LRU_C = 8.0
EPS = 1e-6
N_SEG = 8
LANES = 128
HEADS_PER_STEP = LANES // HEAD_DIM
N_HEAD_PAIRS = N_HEADS // HEADS_PER_STEP
AUG_PER_HEAD = 6
GATE_TILE = 256
GATE_TILES = D_MODEL // GATE_TILE
NEG_BIG = -1e30
VMEM_LIMIT = 56 * 1024 * 1024


def _softplus(y):
    return jnp.maximum(y, 0.0) + jnp.log1p(jnp.exp(-jnp.abs(y)))


def _rms(x, g):
    return x * lax.rsqrt(jnp.mean(x * x, axis=-1, keepdims=True) + EPS) * g


def _split3(c):
    hi = c.astype(BF16)
    r1 = c - hi.astype(F32)
    mid = r1.astype(BF16)
    lo = (r1 - mid.astype(F32)).astype(BF16)
    return hi, mid, lo


def _resident(shape):
    zeros = (0,) * len(shape)
    return pl.BlockSpec(shape, lambda *_: zeros, pipeline_mode=pl.Buffered(1))


def _inproj_kernel(x_ref, g_ref, w_ref, wfl_ref, bf_ref,
                   xa_ref, ga_ref, q_ref, kf_ref, kb_ref, vf_ref, vb_ref,
                   gb_ref, ma_ref, mb_ref, lf_ref):
    xn = _rms(x_ref[...], g_ref[...]).astype(BF16)

    def seg(j):
        return jnp.dot(xn, w_ref[j], preferred_element_type=F32)

    xa_ref[...] = seg(0)
    ga_ref[...] = seg(1).astype(BF16)
    q_ref[...] = (seg(2) * (HEAD_DIM ** -0.5)).astype(BF16)
    k = seg(3)
    kf_ref[...] = k
    kb_ref[...] = k.astype(BF16)
    v = seg(4)
    vf_ref[...] = v
    vb_ref[...] = v.astype(BF16)
    gb_ref[...] = seg(5).astype(BF16)
    ma_ref[...] = seg(6).astype(BF16)
    mb_ref[...] = seg(7).astype(BF16)
    fl = jnp.dot(xn, wfl_ref[...], preferred_element_type=F32) + bf_ref[...]
    lf_ref[...] = (-_softplus(-fl))[:, :N_HEADS]


def _inproj(x2, g_pre, w_seg, w_fl, b_f, tm):
    n = x2.shape[0]
    row = lambda i: (i, 0)
    wide_f32 = jax.ShapeDtypeStruct((n, D_MODEL), F32)
    wide_bf16 = jax.ShapeDtypeStruct((n, D_MODEL), BF16)
    wide_spec = pl.BlockSpec((tm, D_MODEL), row)
    out_shape = (wide_f32, wide_bf16, wide_bf16, wide_f32, wide_bf16, wide_f32, wide_bf16,
                 wide_bf16, wide_bf16, wide_bf16, jax.ShapeDtypeStruct((n, N_HEADS), F32))
    out_specs = (wide_spec,) * 10 + (pl.BlockSpec((tm, N_HEADS), row),)
    return pl.pallas_call(
        _inproj_kernel,
        grid=(n // tm,),
        in_specs=[wide_spec,
                  _resident((1, D_MODEL)),
                  _resident((N_SEG, D_MODEL, D_MODEL)),
                  _resident((D_MODEL, LANES)),
                  _resident((1, LANES))],
        out_specs=out_specs,
        out_shape=out_shape,
        compiler_params=pltpu.CompilerParams(
            dimension_semantics=("arbitrary",), vmem_limit_bytes=VMEM_LIMIT),
        name="inproj",
    )(x2, g_pre, w_seg, w_fl, b_f)


CUM_BLOCK = 256


def _cum_kernel(lf_ref, tri_ref, pq_ref, pk_ref, oq_ref, ok_ref, aq_ref, ak_ref, carry_ref):
    n_blocks = lf_ref.shape[1] // CUM_BLOCK
    carry_ref[...] = jnp.zeros_like(carry_ref)
    tri = tri_ref[...]

    def place(parts, p_ref):
        out = jnp.dot(parts[0], p_ref[0], preferred_element_type=F32)
        out += jnp.dot(parts[1], p_ref[1], preferred_element_type=F32)
        out += jnp.dot(parts[2], p_ref[2], preferred_element_type=F32)
        return out

    def body(i, _):
        rows = pl.ds(pl.multiple_of(i * CUM_BLOCK, CUM_BLOCK), CUM_BLOCK)
        hi, mid, lo = _split3(lf_ref[0, rows, :])
        local = jnp.dot(tri, hi, preferred_element_type=F32)
        local += jnp.dot(tri, mid, preferred_element_type=F32)
        local += jnp.dot(tri, lo, preferred_element_type=F32)
        cum = local + carry_ref[...]
        carry_ref[...] = cum[CUM_BLOCK - 1:CUM_BLOCK, :]
        parts = _split3(cum)
        aq_ref[0, rows, :] = (place(parts, pq_ref) + oq_ref[...]).astype(BF16)
        ak_ref[0, rows, :] = (place(parts, pk_ref) + ok_ref[...]).astype(BF16)
        return 0

    lax.fori_loop(0, n_blocks, body, 0)


def _aug_constants():
    pq = np.zeros((3, N_HEADS, LANES), np.float32)
    pk = np.zeros((3, N_HEADS, LANES), np.float32)
    oq = np.zeros((1, LANES), np.float32)
    ok = np.zeros((1, LANES), np.float32)
    for h in range(N_HEADS):
        for i in range(3):
            pq[i, h, AUG_PER_HEAD * h + i] = 1.0
            ok[0, AUG_PER_HEAD * h + i] = 1.0
            pk[i, h, AUG_PER_HEAD * h + 3 + i] = -1.0
            oq[0, AUG_PER_HEAD * h + 3 + i] = 1.0
    tri = np.tril(np.ones((CUM_BLOCK, CUM_BLOCK), np.float32))
    return (jnp.asarray(tri, BF16), jnp.asarray(pq, BF16), jnp.asarray(pk, BF16),
            jnp.asarray(oq), jnp.asarray(ok))


def _cum_aug(logf):
    b, s, _ = logf.shape
    tri, pq, pk, oq, ok = _aug_constants()
    aug = jax.ShapeDtypeStruct((b, s, LANES), BF16)
    aug_spec = pl.BlockSpec((1, s, LANES), lambda i: (i, 0, 0))
    return pl.pallas_call(
        _cum_kernel,
        grid=(b,),
        in_specs=[pl.BlockSpec((1, s, N_HEADS), lambda i: (i, 0, 0)),
                  _resident((CUM_BLOCK, CUM_BLOCK)),
                  _resident((3, N_HEADS, LANES)),
                  _resident((3, N_HEADS, LANES)),
                  _resident((1, LANES)),
                  _resident((1, LANES))],
        out_specs=(aug_spec, aug_spec),
        out_shape=(aug, aug),
        scratch_shapes=[pltpu.VMEM((1, N_HEADS), F32)],
        compiler_params=pltpu.CompilerParams(
            dimension_semantics=("arbitrary",), vmem_limit_bytes=VMEM_LIMIT),
        name="cum_aug",
    )(logf, tri, pq, pk, oq, ok)


HIST = CONV_W - 1
PAD = 8


def _lru_kernel(xa_ref, ga_ref, hist_ref, h0_ref, cw_ref, cb_ref, wa_ref, ba_ref, wx_ref, bx_ref,
                ap_ref, y_ref, hl_ref, cn_ref, buf_ref, hc_ref):
    t = pl.program_id(1)
    tt = xa_ref.shape[1]

    @pl.when(t == 0)
    def _():
        buf_ref[PAD - HIST:PAD, :] = hist_ref[0]
        hc_ref[...] = h0_ref[0]

    buf_ref[PAD:PAD + tt, :] = xa_ref[0]
    u = cb_ref[...]
    for tap in range(CONV_W):
        u = u + buf_ref[PAD - HIST + tap:PAD - HIST + tap + tt, :] * cw_ref[tap:tap + 1, :]
    tail = buf_ref[PAD + tt - HIST:PAD + tt, :]
    buf_ref[PAD - HIST:PAD, :] = tail
    cn_ref[0] = tail

    ub = u.astype(BF16)

    def gate(w_ref, b_ref):
        cols = [jnp.dot(ub[:, c * GATE_TILE:(c + 1) * GATE_TILE], w_ref[c],
                        preferred_element_type=F32) for c in range(GATE_TILES)]
        return jax.nn.sigmoid(jnp.concatenate(cols, axis=1) + b_ref[...])

    r = gate(wa_ref, ba_ref)
    i = gate(wx_ref, bx_ref)
    log_a = -LRU_C * r * _softplus(-ap_ref[...])
    a = jnp.exp(log_a)
    x_in = jnp.sqrt(jnp.tanh(-log_a) * (a * a + 1.0)) * (i * u)

    row = lax.broadcasted_iota(jnp.int32, (tt, 1), 0)
    d = 1
    while d < tt:
        keep = row >= d
        x_sh = pltpu.roll(x_in, d, 0)
        a_sh = pltpu.roll(a, d, 0)
        x_in = jnp.where(keep, a * x_sh + x_in, x_in)
        a = jnp.where(keep, a * a_sh, a)
        d *= 2
    h = x_in + a * hc_ref[...]
    h_last = h[tt - 1:tt, :]
    hc_ref[...] = h_last
    hl_ref[0] = h_last
    g = ga_ref[0].astype(F32)
    y_ref[0] = (h * (g * jax.nn.sigmoid(g))).astype(BF16)


def _lru(xa, ga, hist, h0, cw, cb, wa, ba, wx, bx, ap, tt):
    b, l, _ = xa.shape
    tile = pl.BlockSpec((1, tt, D_MODEL), lambda i, t: (i, t, 0))
    per_b = lambda rows: pl.BlockSpec((1, rows, D_MODEL), lambda i, t: (i, 0, 0))
    vec = _resident((1, D_MODEL))
    wgt = _resident((GATE_TILES, GATE_TILE, GATE_TILE))
    return pl.pallas_call(
        _lru_kernel,
        grid=(b, l // tt),
        in_specs=[tile, tile, per_b(HIST), per_b(1), _resident((CONV_W, D_MODEL)), vec,
                  wgt, vec, wgt, vec, vec],
        out_specs=(tile, per_b(1), per_b(HIST)),
        out_shape=(jax.ShapeDtypeStruct((b, l, D_MODEL), BF16),
                   jax.ShapeDtypeStruct((b, 1, D_MODEL), F32),
                   jax.ShapeDtypeStruct((b, HIST, D_MODEL), F32)),
        scratch_shapes=[pltpu.VMEM((PAD + tt, D_MODEL), F32), pltpu.VMEM((1, D_MODEL), F32)],
        compiler_params=pltpu.CompilerParams(
            dimension_semantics=("arbitrary", "arbitrary"), vmem_limit_bytes=VMEM_LIMIT),
        name="rg_lru",
    )(xa, ga, hist, h0, cw, cb, wa, ba, wx, bx, ap)


def _fox_kernel(q_ref, aq_ref, k_ref, ak_ref, v_ref, gb_ref, y_ref,
                kcat_ref, vcat_ref, lhs_ref, s_ref, m_ref, acc_ref, alpha_ref, p_ref,
                *, tq, tk, offset):
    hp = pl.program_id(1)
    qi = pl.program_id(2)
    sk = k_ref.shape[1]

    @pl.when(qi == 0)
    def _():
        kcat_ref[:, :LANES] = k_ref[0]
        kcat_ref[:, LANES:] = ak_ref[0]
        vcat_ref[:, :LANES] = v_ref[0]
        vcat_ref[:, LANES:] = jnp.ones((sk, LANES), BF16)

    lane = lax.broadcasted_iota(jnp.int32, (1, LANES), 1)
    qb = q_ref[0]
    ab = aq_ref[0]
    for h in range(HEADS_PER_STEP):
        head = hp * HEADS_PER_STEP + h
        q_keep = (lane >= h * HEAD_DIM) & (lane < (h + 1) * HEAD_DIM)
        a_keep = (lane >= head * AUG_PER_HEAD) & (lane < (head + 1) * AUG_PER_HEAD)
        lhs_ref[h, :, :LANES] = jnp.where(q_keep, qb, jnp.zeros_like(qb))
        lhs_ref[h, :, LANES:] = jnp.where(a_keep, ab, jnp.zeros_like(ab))

    m_ref[...] = jnp.full(m_ref.shape, NEG_BIG, F32)
    acc_ref[...] = jnp.zeros(acc_ref.shape, F32)

    q_first = offset + qi * tq
    n_full = (q_first + 1) // tk
    n_end = (q_first + tq + tk - 1) // tk

    def key_rows(kb):
        if isinstance(kb, int):
            return pl.ds(kb * tk, tk)
        return pl.ds(pl.multiple_of(kb * tk, tk), tk)

    def logits(kb, slot):
        kblk = kcat_ref[key_rows(kb), :]
        for h in range(HEADS_PER_STEP):
            s_ref[slot, h] = lax.dot_general(lhs_ref[h], kblk, (((1,), (1,)), ((), ())),
                                             preferred_element_type=F32)

    def softmax(kb, slot, masked):
        if masked:
            qpos = q_first + lax.broadcasted_iota(jnp.int32, (tq, tk), 0)
            kpos = kb * tk + lax.broadcasted_iota(jnp.int32, (tq, tk), 1)
            visible = kpos <= qpos
        for h in range(HEADS_PER_STEP):
            s = s_ref[slot, h]
            if masked:
                s = jnp.where(visible, s, NEG_BIG)
            m_prev = m_ref[h]
            m_new = jnp.maximum(m_prev, jnp.max(s, axis=1, keepdims=True))
            alpha_ref[slot, h] = jnp.exp(m_prev - m_new)
            p_ref[slot, h] = jnp.exp(s - jnp.tile(m_new, (1, tk // LANES))).astype(BF16)
            m_ref[h] = m_new

    def values(kb, slot):
        vblk = vcat_ref[key_rows(kb), :]
        for h in range(HEADS_PER_STEP):
            pv = jnp.dot(p_ref[slot, h], vblk, preferred_element_type=F32)
            acc_ref[h] = jnp.tile(alpha_ref[slot, h], (1, 2)) * acc_ref[h] + pv

    def body(j, slot):
        values(j - 2, slot)
        softmax(j - 1, 1 - slot, False)
        logits(j, slot)

    def pair(jj, c):
        body(2 + 2 * jj, 0)
        body(3 + 2 * jj, 1)
        return c

    last = n_end - 1
    n_body = jnp.maximum(n_end - 2, 0)

    logits(0, 0)

    @pl.when(n_end == 1)
    def _():
        softmax(0, 0, True)
        values(0, 0)

    @pl.when(n_end > 1)
    def _():
        softmax(0, 0, False)
        logits(1, 1)

    lax.fori_loop(0, n_body // 2, pair, 0)

    @pl.when(n_body % 2 == 1)
    def _():
        body(last, 0)

    def drain(slot):
        values(last - 1, 1 - slot)
        softmax(last, slot, True)
        values(last, slot)

    @pl.when((n_end > 1) & (last % 2 == 0))
    def _():
        drain(0)

    @pl.when((n_end > 1) & (last % 2 == 1))
    def _():
        drain(1)

    outs = [acc_ref[h][:, :LANES] / acc_ref[h][:, LANES:] for h in range(HEADS_PER_STEP)]
    o = jnp.where(lane < HEAD_DIM, outs[0], outs[1])
    g = gb_ref[0].astype(F32)
    y_ref[0] = (o * (g * jax.nn.sigmoid(g))).astype(BF16)


def _fox(q, aq, k, ak, v, gb, tq, tk, offset):
    b, sq, _ = q.shape
    sk = k.shape[1]
    assert tk % tq == 0 and offset % tk == 0 and sq % tq == 0 and sk % tk == 0
    q_spec = pl.BlockSpec((1, tq, LANES), lambda i, hp, qi: (i, qi, hp))
    aq_spec = pl.BlockSpec((1, tq, LANES), lambda i, hp, qi: (i, qi, 0))
    k_spec = pl.BlockSpec((1, sk, LANES), lambda i, hp, qi: (i, 0, hp))
    ak_spec = pl.BlockSpec((1, sk, LANES), lambda i, hp, qi: (i, 0, 0))
    return pl.pallas_call(
        functools.partial(_fox_kernel, tq=tq, tk=tk, offset=offset),
        grid=(b, N_HEAD_PAIRS, sq // tq),
        in_specs=[q_spec, aq_spec, k_spec, ak_spec, k_spec, q_spec],
        out_specs=q_spec,
        out_shape=jax.ShapeDtypeStruct((b, sq, D_MODEL), BF16),
        scratch_shapes=[pltpu.VMEM((sk, 2 * LANES), BF16),
                        pltpu.VMEM((sk, 2 * LANES), BF16),
                        pltpu.VMEM((HEADS_PER_STEP, tq, 2 * LANES), BF16),
                        pltpu.VMEM((2, HEADS_PER_STEP, tq, tk), F32),
                        pltpu.VMEM((HEADS_PER_STEP, tq, LANES), F32),
                        pltpu.VMEM((HEADS_PER_STEP, tq, 2 * LANES), F32),
                        pltpu.VMEM((2, HEADS_PER_STEP, tq, LANES), F32),
                        pltpu.VMEM((2, HEADS_PER_STEP, tq, tk), BF16)],
        compiler_params=pltpu.CompilerParams(
            dimension_semantics=("arbitrary", "arbitrary", "arbitrary"),
            vmem_limit_bytes=VMEM_LIMIT),
        name="fox_attention",
    )(q, aq, k, ak, v, gb)


def _post_kernel(ya_ref, yb_ref, ma_ref, mb_ref, x_ref, pe_ref, wb_ref, wo_ref, gpost_ref,
                 wg_ref, wp_ref, gple_ref, out_ref):
    ua = jnp.dot(ya_ref[...], wb_ref[0], preferred_element_type=F32)
    ub = jnp.dot(yb_ref[...], wb_ref[1], preferred_element_type=F32)
    mixed = (jax.nn.sigmoid(ma_ref[...].astype(F32)) * ua
             + jax.nn.sigmoid(mb_ref[...].astype(F32)) * ub)
    mix = jnp.dot(mixed.astype(BF16), wo_ref[...], preferred_element_type=F32)
    h = x_ref[...] + _rms(mix, gpost_ref[...])
    gate = jax.nn.sigmoid(jnp.dot(h.astype(BF16), wg_ref[...], preferred_element_type=F32))
    proj = jnp.dot(pe_ref[...].astype(BF16), wp_ref[...], preferred_element_type=F32)
    out_ref[...] = h + _rms(gate * proj, gple_ref[...])


def _post(ya, yb, ma, mb, x2, pe2, wb, wo, gpost, wg, wp, gple, tm):
    n = x2.shape[0]
    ple_dim = pe2.shape[1]
    row = lambda i: (i, 0)
    wide = pl.BlockSpec((tm, D_MODEL), row)
    sq_w = _resident((D_MODEL, D_MODEL))
    vec = _resident((1, D_MODEL))
    return pl.pallas_call(
        _post_kernel,
        grid=(n // tm,),
        in_specs=[wide, wide, wide, wide, wide, pl.BlockSpec((tm, ple_dim), row),
                  _resident((2, D_MODEL, D_MODEL)), sq_w, vec, sq_w,
                  _resident((ple_dim, D_MODEL)), vec],
        out_specs=wide,
        out_shape=jax.ShapeDtypeStruct((n, D_MODEL), F32),
        compiler_params=pltpu.CompilerParams(
            dimension_semantics=("arbitrary",), vmem_limit_bytes=VMEM_LIMIT),
        name="merge_post",
    )(ya, yb, ma, mb, x2, pe2, wb, wo, gpost, wg, wp, gple)


def _block_diag_tiles(w):
    per_tile = GATE_TILE // BLOCK_A
    w = w.reshape(GATE_TILES, per_tile, BLOCK_A, BLOCK_A)
    eye = jnp.eye(per_tile, dtype=w.dtype)
    tiles = jnp.einsum('cajk,ab->cajbk', w, eye)
    return tiles.reshape(GATE_TILES, GATE_TILE, GATE_TILE).astype(BF16)


def _prep_weights(w_in, conv_w, conv_b, w_rg_a, b_rg_a, w_rg_x, b_rg_x, a_param, b_f, w_branch,
                  w_o, g_pre, g_post, w_ple_gate, w_ple_proj, g_ple):
    d = D_MODEL
    starts = (0, d, 2 * d, 3 * d, 4 * d, 5 * d, 6 * d + N_HEADS, 7 * d + N_HEADS)
    w_seg = jnp.stack([w_in[:, s:s + d] for s in starts]).astype(BF16)
    w_fl = jnp.pad(w_in[:, 6 * d:6 * d + N_HEADS], ((0, 0), (0, LANES - N_HEADS))).astype(BF16)
    b_fl = jnp.pad(b_f, (0, LANES - N_HEADS)).reshape(1, LANES)
    vec = lambda a: a.reshape(1, d)
    return dict(
        g_pre=vec(g_pre), w_seg=w_seg, w_fl=w_fl, b_fl=b_fl,
        cw=conv_w, cb=vec(conv_b), wa=_block_diag_tiles(w_rg_a), ba=vec(b_rg_a),
        wx=_block_diag_tiles(w_rg_x), bx=vec(b_rg_x), ap=vec(a_param),
        wb=w_branch.astype(BF16), wo=w_o.astype(BF16), gpost=vec(g_post),
        wg=w_ple_gate.astype(BF16), wp=w_ple_proj.astype(BF16), gple=vec(g_ple))


def _round_up(n, m):
    return (n + m - 1) // m * m


def _layer(x, pe, conv_hist, h0, k_past, v_past, lf_past, p):
    b, l, d = x.shape
    n = b * l
    tm = min(256, n)
    x2 = x.reshape(n, d)
    xa, ga, q, k_f, k_b, v_f, v_b, gb, ma, mb, logf = _inproj(
        x2, p["g_pre"], p["w_seg"], p["w_fl"], p["b_fl"], tm)
    r3 = lambda a: a.reshape(b, l, d)

    ya, h_new, conv_new = _lru(r3(xa), r3(ga), conv_hist, h0.reshape(b, 1, d), p["cw"], p["cb"],
                               p["wa"], p["ba"], p["wx"], p["bx"], p["ap"], min(256, l))

    logf3 = logf.reshape(b, l, N_HEADS)
    tk = 512
    if k_past is None:
        past = 0
        lf_all, k_all, v_all = logf3, r3(k_b), r3(v_b)
        tq = min(512, l)
    else:
        past = k_past.shape[1]
        sk = _round_up(past + l, tk)
        tail = sk - past - l
        cat = lambda old, new: jnp.concatenate(
            [old.reshape(b, past, -1).astype(new.dtype), new,
             jnp.zeros((b, tail, new.shape[-1]), new.dtype)], axis=1)
        lf_all, k_all, v_all = cat(lf_past, logf3), cat(k_past, r3(k_b)), cat(v_past, r3(v_b))
        tq = l
    aq, ak = _cum_aug(lf_all)
    aq = aq[:, past:past + l]
    yb = _fox(r3(q), aq, k_all, ak, v_all, r3(gb), tq, tk, past)

    out = _post(ya.reshape(n, d), yb.reshape(n, d), ma, mb, x2, pe.reshape(n, -1),
                p["wb"], p["wo"], p["gpost"], p["wg"], p["wp"], p["gple"], tm)
    heads = lambda a: a.reshape(b, l, N_HEADS, HEAD_DIM)
    return out.reshape(b, l, d), (heads(k_f), heads(v_f), logf3, h_new.reshape(b, d), conv_new)


def kernel(x_prompt, x_sample, p_prompt, p_sample, cache_k, cache_v, cache_logf, state_h, state_conv,
           w_in, conv_w, conv_b, w_rg_a, b_rg_a, w_rg_x, b_rg_x, a_param, b_f, w_branch, w_o,
           g_pre, g_post, w_ple_gate, w_ple_proj, g_ple):
    depth = w_in.shape[0]
    hp, hs = x_prompt, x_sample
    outs_p, outs_s = [], []
    for i in range(depth):
        p = _prep_weights(w_in[i], conv_w[i], conv_b[i], w_rg_a[i], b_rg_a[i], w_rg_x[i], b_rg_x[i],
                          a_param[i], b_f[i], w_branch[i], w_o[i], g_pre[i], g_post[i],
                          w_ple_gate[i], w_ple_proj[i], g_ple[i])
        bp = hp.shape[0]
        hp, st_p = _layer(hp, p_prompt[i], jnp.zeros((bp, HIST, D_MODEL), hp.dtype),
                          jnp.zeros((bp, D_MODEL), F32), None, None, None, p)
        hs, st_s = _layer(hs, p_sample[i], state_conv[i], state_h[i],
                          cache_k[i], cache_v[i], cache_logf[i], p)
        outs_p.append(st_p)
        outs_s.append(st_s)
    stack = lambda outs, j: jnp.stack([o[j] for o in outs])
    return (hp, hs) + tuple(stack(outs_p, j) for j in range(5)) + tuple(stack(outs_s, j) for j in range(5))
```
